```python
import jax, jax.numpy as jnp
from jax import lax
import numpy as np

D_MODEL = 1024
BATCH = 4
SEQ = 4096
DEPTH = 1
DEC_BATCH = 32
DEC_SEQ = 8
PAST_LEN = 16384
PAGE_SIZE = 128

H_A = 8
DH_A = 64
D_A = H_A * DH_A
MOBA_BLOCK = 256
MOBA_TOPK = 3
Q_BLK = 128
ROPE_THETA = 10000.0
H_B = 8
DK_B = 64
DV_B = 64
D_B = H_B * DV_B
D_CONV_B = 2 * H_B * DK_B + D_B
CONV_B = 4
GDN_CHUNK = 64
D_FF = 2816
CONV_F = 3
EPS = 1e-6
N_IN = 3 * D_A + D_CONV_B + D_B + 2 * H_B + 2 * D_MODEL
SPLITS = (D_A, 2 * D_A, 3 * D_A, 3 * D_A + D_CONV_B, 3 * D_A + D_CONV_B + D_B,
          3 * D_A + D_CONV_B + D_B + H_B, 3 * D_A + D_CONV_B + D_B + 2 * H_B,
          3 * D_A + D_CONV_B + D_B + 2 * H_B + D_MODEL)

kernel_name = 'moba_gdn_parallel_convffn_adaln_step'


def rmsnorm(x, g):
    xf = x.astype(jnp.float32)
    y = xf * lax.rsqrt(jnp.mean(xf * xf, axis=-1, keepdims=True) + EPS)
    return (y * g.astype(jnp.float32)).astype(x.dtype)


def l2norm(x):
    xf = x.astype(jnp.float32)
    return xf * lax.rsqrt(jnp.sum(xf * xf, axis=-1, keepdims=True) + EPS)


def rope(x, pos):
    half = DH_A // 2
    inv = ROPE_THETA ** (-jnp.arange(half, dtype=jnp.float32) / half)
    ang = pos.astype(jnp.float32)[:, None] * inv[None, :]
    cos = jnp.cos(ang)[None, :, None, :]
    sin = jnp.sin(ang)[None, :, None, :]
    xf = x.astype(jnp.float32)
    x1, x2 = xf[..., :half], xf[..., half:]
    return jnp.concatenate([x1 * cos - x2 * sin, x2 * cos + x1 * sin], axis=-1).astype(x.dtype)


def causal_dwconv(x, w, buf):
    width, length = w.shape[0], x.shape[1]
    xx = jnp.concatenate([buf.astype(x.dtype), x], axis=1)
    y = xx[:, 0:length] * w[0]
    for j in range(1, width):
        y = y + xx[:, j:j + length] * w[j]
    return y, xx[:, length:]


def moba_select(q, kmean, n_past):
    s = jnp.einsum('...qhd,...nhd->...qhn', q.astype(jnp.float32), kmean)
    nb = kmean.shape[-3]
    allowed = jnp.arange(nb)[None, :] < n_past[:, None]
    s = jnp.where(allowed[:, None, :], s, -jnp.inf)
    _, idx = lax.top_k(s, MOBA_TOPK)
    valid = jnp.arange(MOBA_TOPK)[None, :] < n_past[:, None]
    return idx, valid


def moba_softmax(q, k_sel, v_sel, sel_mask, k_own, v_own, own_mask):
    scale = DH_A ** -0.5
    s_sel = jnp.einsum('...qhd,...qhsd->...qhs', q, k_sel, preferred_element_type=jnp.float32) * scale
    s_own = jnp.einsum('...qhd,...thd->...qht', q, k_own, preferred_element_type=jnp.float32) * scale
    s_sel = jnp.where(sel_mask[:, None, :], s_sel, -jnp.inf)
    s_own = jnp.where(own_mask[:, None, :], s_own, -jnp.inf)
    p = jax.nn.softmax(jnp.concatenate([s_sel, s_own], axis=-1), axis=-1).astype(v_own.dtype)
    n_sel = k_sel.shape[-2]
    return (jnp.einsum('...qhs,...qhsd->...qhd', p[..., :n_sel], v_sel)
            + jnp.einsum('...qht,...thd->...qhd', p[..., n_sel:], v_own))


def moba_prompt(q, k, v):
    B, S = q.shape[:2]
    nb = max(-(-S // MOBA_BLOCK), MOBA_TOPK)
    padw = ((0, 0), (0, nb * MOBA_BLOCK - S), (0, 0), (0, 0))
    kblk = jnp.pad(k, padw).reshape(B, nb, MOBA_BLOCK, H_A, DH_A)
    vblk = jnp.pad(v, padw).reshape(B, nb, MOBA_BLOCK, H_A, DH_A)
    kmean = jnp.mean(kblk.astype(jnp.float32), axis=2)
    nq = S // Q_BLK
    qc = q.reshape(B * nq, Q_BLK, H_A, DH_A)
    bidx = jnp.repeat(jnp.arange(B), nq)
    cidx = jnp.tile(jnp.arange(nq), B)
    hix = jnp.arange(H_A)[None, :, None]
    blk_pos = jnp.arange(MOBA_BLOCK)

    def one_block(args):
        qb, b, ci = args
        qpos = ci * Q_BLK + jnp.arange(Q_BLK)
        ob = (ci * Q_BLK) // MOBA_BLOCK
        kb_, vb_ = kblk[b], vblk[b]
        idx, valid = moba_select(qb, kmean[b], jnp.full((Q_BLK,), ob))
        k_sel = kb_[idx, :, hix].reshape(Q_BLK, H_A, MOBA_TOPK * MOBA_BLOCK, DH_A)
        v_sel = vb_[idx, :, hix].reshape(Q_BLK, H_A, MOBA_TOPK * MOBA_BLOCK, DH_A)
        sel_mask = jnp.repeat(valid, MOBA_BLOCK, axis=-1)
        own_mask = (ob * MOBA_BLOCK + blk_pos)[None, :] <= qpos[:, None]
        return moba_softmax(qb, k_sel, v_sel, sel_mask, kb_[ob], vb_[ob], own_mask)

    o = lax.map(one_block, (qc, bidx, cidx))
    return o.reshape(B, S, H_A, DH_A)


def moba_sample(q, k, v, cache_k, cache_v, page_table, layer):
    Bd, L = q.shape[:2]
    ppb = MOBA_BLOCK // PAGE_SIZE
    n_pages = PAST_LEN // PAGE_SIZE
    nb_past = PAST_LEN // MOBA_BLOCK
    nb = max(nb_past, MOBA_TOPK)
    past_pages = page_table[:, :nb_past * ppb]
    kpast = cache_k[layer, past_pages].astype(jnp.float32).reshape(Bd, nb_past, MOBA_BLOCK, H_A, DH_A)
    kmean = jnp.pad(jnp.mean(kpast, axis=2), ((0, 0), (0, nb - nb_past), (0, 0), (0, 0)))
    idx, valid = moba_select(q, kmean, jnp.full((L,), nb_past))
    lpage = jnp.minimum(idx[..., None] * ppb + jnp.arange(ppb), n_pages - 1)
    phys = page_table[jnp.arange(Bd)[:, None, None, None, None], lpage]
    hix = jnp.arange(H_A)[None, None, :, None, None, None]
    rows = jnp.arange(PAGE_SIZE)
    k_sel = cache_k[layer, phys[..., None], rows, hix].reshape(Bd, L, H_A, MOBA_TOPK * MOBA_BLOCK, DH_A)
    v_sel = cache_v[layer, phys[..., None], rows, hix].reshape(Bd, L, H_A, MOBA_TOPK * MOBA_BLOCK, DH_A)
    sel_mask = jnp.repeat(valid, MOBA_BLOCK, axis=-1)
    own_pages = page_table[:, nb_past * ppb:]
    n_own = n_pages - nb_past * ppb
    k_own = jnp.concatenate([cache_k[layer, own_pages].reshape(Bd, n_own * PAGE_SIZE, H_A, DH_A).astype(k.dtype), k], axis=1)
    v_own = jnp.concatenate([cache_v[layer, own_pages].reshape(Bd, n_own * PAGE_SIZE, H_A, DH_A).astype(v.dtype), v], axis=1)
    kpos = nb_past * MOBA_BLOCK + jnp.arange(n_own * PAGE_SIZE + L)
    qpos = PAST_LEN + jnp.arange(L)
    own_mask = kpos[None, :] <= qpos[:, None]
    return moba_softmax(q, k_sel.astype(q.dtype), v_sel.astype(q.dtype), sel_mask, k_own, v_own, own_mask)


def gdn_chunked(q, k, v, beta, g, s0):
    B, L = q.shape[:2]
    C = min(GDN_CHUNK, L)
    n = -(-L // C)
    pad = n * C - L

    def blocks(t):
        t = jnp.pad(t.astype(jnp.float32), ((0, 0), (0, pad)) + ((0, 0),) * (t.ndim - 2))
        t = t.reshape((B, n, C) + t.shape[2:])
        return jnp.moveaxis(t, 3, 2)

    qc, kc, vc, bc, gc = blocks(q), blocks(k), blocks(v), blocks(beta), blocks(g)
    gcum = jnp.cumsum(gc, axis=-1)
    ii = jnp.arange(C)
    tril = ii[:, None] >= ii[None, :]
    strict = ii[:, None] > ii[None, :]
    decay = jnp.exp(jnp.where(tril, gcum[..., :, None] - gcum[..., None, :], -jnp.inf))
    kbeta = kc * bc[..., None]
    m = jnp.where(strict, jnp.einsum('bnhid,bnhjd->bnhij', kbeta, kc) * decay, 0.0)
    rhs = jnp.concatenate([vc * bc[..., None], kbeta * jnp.exp(gcum)[..., None]], axis=-1)
    sol = lax.linalg.triangular_solve(m, rhs, left_side=True, lower=True, unit_diagonal=True)
    u, w = sol[..., :DV_B], sol[..., DV_B:]
    qk = jnp.where(tril, jnp.einsum('bnhid,bnhjd->bnhij', qc, kc) * decay, 0.0)
    g_last = gcum[..., -1]
    k_dec = kc * jnp.exp(g_last[..., None] - gcum)[..., None]
    q_dec = qc * jnp.exp(gcum)[..., None]
    xs = (jnp.moveaxis(u, 1, 0), jnp.moveaxis(w, 1, 0), jnp.moveaxis(q_dec, 1, 0),
          jnp.moveaxis(qk, 1, 0), jnp.moveaxis(k_dec, 1, 0), jnp.moveaxis(g_last, 1, 0))

    def step(S, xs_i):
        u_i, w_i, q_i, qk_i, kd_i, gl_i = xs_i
        v_new = u_i - jnp.einsum('bhcd,bhde->bhce', w_i, S)
        o_i = jnp.einsum('bhcd,bhde->bhce', q_i, S) + jnp.einsum('bhij,bhje->bhie', qk_i, v_new)
        S = S * jnp.exp(gl_i)[..., None, None] + jnp.einsum('bhcd,bhce->bhde', kd_i, v_new)
        return S, o_i

    s_fin, o = lax.scan(step, s0.astype(jnp.float32), xs)
    o = jnp.moveaxis(jnp.moveaxis(o, 0, 1), 2, 3).reshape(B, n * C, H_B, DV_B)[:, :L]
    return o.astype(v.dtype), s_fin


def layer_forward(x, c, pos, attend, gdn_s0, gdn_buf, ffn_buf, prm):
    B, L, _ = x.shape
    mod = jnp.einsum('bd,de->be', jax.nn.silu(c), prm['w_ada']) + prm['b_ada']
    sh1, sc1, gt1, sh2, sc2, gt2 = [m_[:, None, :] for m_ in jnp.split(mod, 6, axis=-1)]
    h = rmsnorm(x, prm['g_norm1']) * (1 + sc1) + sh1
    proj = jnp.einsum('bld,de->ble', h, prm['w_in'])
    qa, ka, va, qkv_b, z_b, beta_raw, a_raw, gate_a, gate_b = jnp.split(proj, SPLITS, axis=-1)
    qa = rope(qa.reshape(B, L, H_A, DH_A), pos)
    ka = rope(ka.reshape(B, L, H_A, DH_A), pos)
    va = va.reshape(B, L, H_A, DH_A)
    o_a = attend(qa, ka, va)
    y_a = jnp.einsum('ble,ed->bld', o_a.reshape(B, L, D_A), prm['w_proj_a'])
    qkv_b, gdn_buf_new = causal_dwconv(qkv_b, prm['conv_b_w'], gdn_buf)
    qkv_b = jax.nn.silu(qkv_b)
    qb, kb, vb = jnp.split(qkv_b, [H_B * DK_B, 2 * H_B * DK_B], axis=-1)
    qb = l2norm(qb.reshape(B, L, H_B, DK_B)) * (DK_B ** -0.5)
    kb = l2norm(kb.reshape(B, L, H_B, DK_B))
    vb = vb.reshape(B, L, H_B, DV_B)
    beta = jax.nn.sigmoid(beta_raw.astype(jnp.float32))
    g = -jnp.exp(prm['a_log_b'].astype(jnp.float32)) * jax.nn.softplus(
        a_raw.astype(jnp.float32) + prm['dt_bias_b'].astype(jnp.float32))
    o_b, s_new = gdn_chunked(qb, kb, vb, beta, g, gdn_s0)
    o_b = rmsnorm(o_b, prm['gdn_norm_w']) * jax.nn.silu(z_b.reshape(B, L, H_B, DV_B))
    y_b = jnp.einsum('ble,ed->bld', o_b.reshape(B, L, D_B), prm['w_proj_b'])
    merged = jax.nn.sigmoid(gate_a) * y_a + jax.nn.sigmoid(gate_b) * y_b
    x = x + gt1 * jnp.einsum('bld,de->ble', merged, prm['w_out'])
    h2 = rmsnorm(x, prm['g_norm2']) * (1 + sc2) + sh2
    up = jnp.einsum('bld,df->blf', h2, prm['w_up'])
    up_c, ffn_buf_new = causal_dwconv(up, prm['conv_f_w'], ffn_buf)
    u_g, u_v = jnp.split(up_c + prm['conv_f_b'], 2, axis=-1)
    x = x + gt2 * jnp.einsum('blf,fd->bld', jax.nn.silu(u_g) * u_v, prm['w_down'])
    return x, ka, va, s_new, gdn_buf_new, ffn_buf_new


def setup_inputs(seed: int = 0) -> dict:
    key = jax.random.key(seed)
    ks = jax.random.split(key, 32)
    n_pages = PAST_LEN // PAGE_SIZE
    n_pool = (DEC_BATCH * n_pages * 5) // 4

    def nrm(k, shape, s):
        return jax.random.normal(k, shape, jnp.float32) * s

    inp = {}
    inp['x_prompt'] = nrm(ks[0], (BATCH, SEQ, D_MODEL), 1.0)
    inp['x_sample'] = nrm(ks[1], (DEC_BATCH, DEC_SEQ, D_MODEL), 1.0)
    inp['cache_k'] = nrm(ks[2], (DEPTH, n_pool, PAGE_SIZE, H_A, DH_A), 1.0)
    inp['cache_v'] = nrm(ks[3], (DEPTH, n_pool, PAGE_SIZE, H_A, DH_A), 1.0)
    inp['state_gdn'] = nrm(ks[4], (DEPTH, DEC_BATCH, H_B, DK_B, DV_B), 0.1)
    inp['state_gdn_conv'] = nrm(ks[5], (DEPTH, DEC_BATCH, CONV_B - 1, D_CONV_B), 1.0)
    inp['state_ffn_conv'] = nrm(ks[6], (DEPTH, DEC_BATCH, CONV_F - 1, 2 * D_FF), 1.0)
    inp['page_table'] = jax.random.permutation(ks[7], n_pool)[:DEC_BATCH * n_pages].reshape(
        DEC_BATCH, n_pages).astype(jnp.int32)
    inp['c_prompt'] = nrm(ks[8], (BATCH, D_MODEL), 1.0)
    inp['c_sample'] = nrm(ks[9], (DEC_BATCH, D_MODEL), 1.0)
    inp['w_ada'] = nrm(ks[10], (DEPTH, D_MODEL, 6 * D_MODEL), 0.5 * D_MODEL ** -0.5)
    inp['b_ada'] = nrm(ks[11], (DEPTH, 6 * D_MODEL), 0.02)
    inp['g_norm1'] = 1.0 + nrm(ks[12], (DEPTH, D_MODEL), 0.02)
    inp['w_in'] = nrm(ks[13], (DEPTH, D_MODEL, N_IN), D_MODEL ** -0.5)
    inp['conv_b_w'] = nrm(ks[14], (DEPTH, CONV_B, D_CONV_B), CONV_B ** -0.5)
    inp['a_log_b'] = jnp.log(jax.random.uniform(ks[15], (DEPTH, H_B), jnp.float32, 1.0, 16.0))
    inp['dt_bias_b'] = jnp.log(jnp.expm1(jax.random.uniform(ks[16], (DEPTH, H_B), jnp.float32, 0.001, 0.1)))
    inp['gdn_norm_w'] = 1.0 + nrm(ks[17], (DEPTH, DV_B), 0.02)
    inp['w_proj_a'] = nrm(ks[18], (DEPTH, D_A, D_MODEL), D_A ** -0.5)
    inp['w_proj_b'] = nrm(ks[19], (DEPTH, D_B, D_MODEL), D_B ** -0.5)
    inp['w_out'] = nrm(ks[20], (DEPTH, D_MODEL, D_MODEL), D_MODEL ** -0.5)
    inp['g_norm2'] = 1.0 + nrm(ks[21], (DEPTH, D_MODEL), 0.02)
    inp['w_up'] = nrm(ks[22], (DEPTH, D_MODEL, 2 * D_FF), D_MODEL ** -0.5)
    inp['conv_f_w'] = nrm(ks[23], (DEPTH, CONV_F, 2 * D_FF), CONV_F ** -0.5)
    inp['conv_f_b'] = nrm(ks[24], (DEPTH, 2 * D_FF), 0.02)
    inp['w_down'] = nrm(ks[25], (DEPTH, D_FF, D_MODEL), D_FF ** -0.5)
    inp['g_norm_f'] = 1.0 + nrm(ks[26], (D_MODEL,), 0.02)
    return inp


def reference(x_prompt, x_sample, cache_k, cache_v, state_gdn, state_gdn_conv, state_ffn_conv, page_table,
              c_prompt, c_sample, w_ada, b_ada, g_norm1, w_in, conv_b_w, a_log_b, dt_bias_b, gdn_norm_w,
              w_proj_a, w_proj_b, w_out, g_norm2, w_up, conv_f_w, conv_f_b, w_down, g_norm_f):
    layer_w = {'w_ada': w_ada, 'b_ada': b_ada, 'g_norm1': g_norm1, 'w_in': w_in, 'conv_b_w': conv_b_w,
               'a_log_b': a_log_b, 'dt_bias_b': dt_bias_b, 'gdn_norm_w': gdn_norm_w, 'w_proj_a': w_proj_a,
               'w_proj_b': w_proj_b, 'w_out': w_out, 'g_norm2': g_norm2, 'w_up': w_up, 'conv_f_w': conv_f_w,
               'conv_f_b': conv_f_b, 'w_down': w_down}
    Bp, Lp = x_prompt.shape[:2]
    pos_p = jnp.arange(Lp)
    pos_s = PAST_LEN + jnp.arange(x_sample.shape[1])
    hp, hs = x_prompt, x_sample
    kp_l, vp_l, ks_l, vs_l, sp_l, ss_l, cbp_l, cbs_l, cfp_l, cfs_l = [], [], [], [], [], [], [], [], [], []
    for l in range(DEPTH):
        prm = {name: arr[l] for name, arr in layer_w.items()}
        gdn_s0 = jnp.zeros((Bp, H_B, DK_B, DV_B), jnp.float32)
        gdn_buf0 = jnp.zeros((Bp, CONV_B - 1, D_CONV_B), x_prompt.dtype)
        ffn_buf0 = jnp.zeros((Bp, CONV_F - 1, 2 * D_FF), x_prompt.dtype)
        hp, kp, vp, sp, cbp, cfp = layer_forward(hp, c_prompt, pos_p, moba_prompt, gdn_s0, gdn_buf0, ffn_buf0, prm)

        def attend_s(q, k, v, layer=l):
            return moba_sample(q, k, v, cache_k, cache_v, page_table, layer)

        hs, ks_, vs_, ss, cbs, cfs = layer_forward(hs, c_sample, pos_s, attend_s, state_gdn[l],
                                                   state_gdn_conv[l], state_ffn_conv[l], prm)
        kp_l.append(kp); vp_l.append(vp); ks_l.append(ks_); vs_l.append(vs_)
        sp_l.append(sp); ss_l.append(ss); cbp_l.append(cbp); cbs_l.append(cbs)
        cfp_l.append(cfp); cfs_l.append(cfs)
    y_prompt = rmsnorm(hp, g_norm_f)
    y_sample = rmsnorm(hs, g_norm_f)
    return (y_prompt, y_sample, jnp.stack(kp_l), jnp.stack(vp_l), jnp.stack(ks_l), jnp.stack(vs_l),
            jnp.stack(sp_l), jnp.stack(ss_l), jnp.stack(cbp_l), jnp.stack(cbs_l), jnp.stack(cfp_l), jnp.stack(cfs_l))
```

```python
import functools

import jax
import jax.numpy as jnp
from jax import lax
from jax.experimental import pallas as pl
from jax.experimental.pallas import tpu as pltpu

F32 = jnp.float32
BF16 = jnp.bfloat16

D_MODEL = 1024
PAST_LEN = 16384
PAGE_SIZE = 128
H_A = 8
DH_A = 64
D_A = H_A * DH_A
MOBA_BLOCK = 256
MOBA_TOPK = 3
Q_BLK = 128
ROPE_THETA = 10000.0
H_B = 8
DK_B = 64
DV_B = 64
D_B = H_B * DV_B
D_CONV_B = 2 * H_B * DK_B + D_B
CONV_B = 4
GDN_CHUNK = 64
D_FF = 2816
CONV_F = 3
EPS = 1e-6

LANES = 128
SUBLANES = 8
VMEM_LIMIT = 56 * 1024 * 1024

N_MAIN = 3 * D_A + D_CONV_B + D_B + 2 * D_MODEL
N_INP = N_MAIN + LANES

NEG_INF = float("-inf")


def _bdot(a, b):
    return jnp.dot(a.astype(BF16), b.astype(BF16), preferred_element_type=F32)


def _bdot_nt(a, b):
    return lax.dot_general(a.astype(BF16), b.astype(BF16), (((1,), (1,)), ((), ())),
                           preferred_element_type=F32)


def _bdot_tn(a, b):
    return lax.dot_general(a.astype(BF16), b.astype(BF16), (((0,), (0,)), ((), ())),
                           preferred_element_type=F32)


def _sigmoid(x):
    return 1.0 / (1.0 + jnp.exp(-x))


def _silu(x):
    return x * _sigmoid(x)


def _iota(shape, dim):
    return lax.broadcasted_iota(jnp.int32, shape, dim)


def _params(sem):
    return pltpu.CompilerParams(dimension_semantics=sem, vmem_limit_bytes=VMEM_LIMIT)


def _const_spec(shape):
    nd = len(shape)
    return pl.BlockSpec(shape, lambda *_: (0,) * nd, pipeline_mode=pl.Buffered(1))


def _ada_body(c_ref, w_ref, b_ref, o_ref):
    o_ref[...] = _bdot(_silu(c_ref[...]), w_ref[...]) + b_ref[...]


def _ada(c_all, w_ada, b_ada):
    rows = c_all.shape[0]
    n = w_ada.shape[1]
    tn = 1024
    return pl.pallas_call(
        _ada_body,
        grid=(n // tn,),
        in_specs=[pl.BlockSpec((rows, D_MODEL), lambda j: (0, 0)),
                  pl.BlockSpec((D_MODEL, tn), lambda j: (0, j)),
                  pl.BlockSpec((1, tn), lambda j: (0, j))],
        out_specs=pl.BlockSpec((rows, tn), lambda j: (0, j)),
        out_shape=jax.ShapeDtypeStruct((rows, n), F32),
        compiler_params=_params(("arbitrary",)),
        name="ada",
    )(c_all, w_ada, b_ada.reshape(1, n))


def _rope128(x, cos, s1, s2):
    return x * cos + pltpu.roll(x, 32, 1) * s1 + pltpu.roll(x, LANES - 32, 1) * s2


def _inproj_body(x_ref, mod_ref, g_ref, w_ref, cos_ref, s1_ref, s2_ref,
                 q_ref, k32_ref, kbf_ref, v32_ref, vt_ref, qkvb_ref, z_ref, ga_ref, gb_ref, ba_ref, km_ref):
    x = x_ref[...]
    m = mod_ref[0]
    sh1 = m[:, 0:D_MODEL]
    sc1 = m[:, D_MODEL:2 * D_MODEL]
    y = x * lax.rsqrt(jnp.mean(x * x, axis=-1, keepdims=True) + EPS) * g_ref[...]
    h = (y * (1.0 + sc1) + sh1).astype(BF16)

    def seg(lo, hi):
        return jnp.dot(h, w_ref[:, lo:hi], preferred_element_type=F32)

    cos, s1, s2 = cos_ref[...], s1_ref[...], s2_ref[...]
    tm = x.shape[0]
    for j in range(D_A // LANES):
        lo = j * LANES
        qj = _rope128(seg(lo, lo + LANES), cos, s1, s2)
        q_ref[:, lo:lo + LANES] = (qj * (DH_A ** -0.5)).astype(q_ref.dtype)
        kj = _rope128(seg(D_A + lo, D_A + lo + LANES), cos, s1, s2)
        k32_ref[:, lo:lo + LANES] = kj
        kbf_ref[:, lo:lo + LANES] = kj.astype(BF16)
        nb = km_ref.shape[1]
        km_ref[0, :, lo:lo + LANES] = jnp.sum(kj.reshape(nb, tm // nb, LANES), axis=1) * (1.0 / (tm // nb))
    v = seg(2 * D_A, 3 * D_A)
    v32_ref[...] = v
    vt_ref[...] = v.T.astype(BF16)
    o = 3 * D_A
    qkvb_ref[...] = seg(o, o + D_CONV_B)
    o += D_CONV_B
    z_ref[...] = seg(o, o + D_B).astype(BF16)
    o += D_B
    ga_ref[...] = seg(o, o + D_MODEL).astype(BF16)
    o += D_MODEL
    gb_ref[...] = seg(o, o + D_MODEL).astype(BF16)
    o += D_MODEL
    ba_ref[...] = seg(o, o + LANES)


def _inproj(x2d, mod3, g1, w_in_r, cos, s1, s2, *, tm, n_pos_tiles, rows_per_mod, q_dtype):
    m_rows = x2d.shape[0]
    nt = m_rows // tm
    nb = max(tm // MOBA_BLOCK, 1)
    rm = mod3.shape[1]

    def row_spec(n):
        return pl.BlockSpec((tm, n), lambda i: (i, 0))

    out_shapes = [
        jax.ShapeDtypeStruct((m_rows, D_A), q_dtype),
        jax.ShapeDtypeStruct((m_rows, D_A), F32),
        jax.ShapeDtypeStruct((m_rows, D_A), BF16),
        jax.ShapeDtypeStruct((m_rows, D_A), F32),
        jax.ShapeDtypeStruct((D_A, m_rows), BF16),
        jax.ShapeDtypeStruct((m_rows, D_CONV_B), F32),
        jax.ShapeDtypeStruct((m_rows, D_B), BF16),
        jax.ShapeDtypeStruct((m_rows, D_MODEL), BF16),
        jax.ShapeDtypeStruct((m_rows, D_MODEL), BF16),
        jax.ShapeDtypeStruct((m_rows, LANES), F32),
        jax.ShapeDtypeStruct((nt, nb, D_A), F32),
    ]
    out_specs = [row_spec(D_A), row_spec(D_A), row_spec(D_A), row_spec(D_A),
                 pl.BlockSpec((D_A, tm), lambda i: (0, i)),
                 row_spec(D_CONV_B), row_spec(D_B), row_spec(D_MODEL), row_spec(D_MODEL), row_spec(LANES),
                 pl.BlockSpec((1, nb, D_A), lambda i: (i, 0, 0))]
    pos_spec = pl.BlockSpec((tm, LANES), lambda i: (i % n_pos_tiles, 0))
    return pl.pallas_call(
        _inproj_body,
        grid=(nt,),
        in_specs=[row_spec(D_MODEL),
                  pl.BlockSpec((1, rm, 6 * D_MODEL), lambda i: (i // rows_per_mod, 0, 0)),
                  _const_spec((1, D_MODEL)),
                  _const_spec((D_MODEL, N_INP)),
                  pos_spec, pos_spec, pos_spec],
        out_specs=out_specs,
        out_shape=out_shapes,
        compiler_params=_params(("arbitrary",)),
        name="inproj",
    )(x2d, mod3, g1, w_in_r, cos, s1, s2)


def _softmax_block(s, m, l, acc, vt):
    m_new = jnp.maximum(m, jnp.max(s, axis=0, keepdims=True))
    alpha = jnp.exp(m - m_new)
    p = jnp.exp(s - m_new)
    l = alpha * l + jnp.sum(p, axis=0, keepdims=True)
    acc = alpha * acc + jnp.dot(vt, p.astype(BF16), preferred_element_type=F32)
    return m_new, l, acc


def _moba_prompt_body(q_ref, k_ref, vt_ref, km_ref, o_ref, sel_ref):
    c = pl.program_id(2)
    ob = c // (MOBA_BLOCK // Q_BLK)
    nq2 = 2 * Q_BLK
    q2 = q_ref[0]
    lane = _iota((Q_BLK, LANES), 1)
    zero = jnp.zeros_like(q2)
    qs = jnp.concatenate([jnp.where(lane < DH_A, q2, zero), jnp.where(lane >= DH_A, q2, zero)], axis=0)

    nblk = km_ref.shape[1]
    ss = _bdot_nt(km_ref[0], qs)
    nidx = _iota((nblk, nq2), 0).astype(F32)
    ss = jnp.where(nidx < ob.astype(F32), ss, NEG_INF)
    sel = jnp.zeros((nblk, nq2), F32)
    for r in range(MOBA_TOPK):
        mx = jnp.max(ss, axis=0, keepdims=True)
        idx = jnp.min(jnp.where(ss == mx, nidx, float(nblk)), axis=0, keepdims=True)
        hit = nidx == idx
        sel = jnp.where(jnp.logical_and(hit, r < ob), 1.0, sel)
        ss = jnp.where(hit, NEG_INF, ss)
    sel_ref[...] = sel

    start = pl.multiple_of(ob * MOBA_BLOCK, MOBA_BLOCK)
    s = _bdot_nt(k_ref[0, pl.ds(start, MOBA_BLOCK), :], qs)
    kpos = _iota((MOBA_BLOCK, nq2), 0)
    qcol = _iota((MOBA_BLOCK, nq2), 1)
    qpos = (c % (MOBA_BLOCK // Q_BLK)) * Q_BLK + jnp.where(qcol >= Q_BLK, qcol - Q_BLK, qcol)
    s = jnp.where(kpos <= qpos, s, NEG_INF)
    m0 = jnp.full((1, nq2), NEG_INF, F32)
    l0 = jnp.zeros((1, nq2), F32)
    acc0 = jnp.zeros((LANES, nq2), F32)
    carry = _softmax_block(s, m0, l0, acc0, vt_ref[0, :, pl.ds(start, MOBA_BLOCK)])

    def past(n, carry):
        st = pl.multiple_of(n * MOBA_BLOCK, MOBA_BLOCK)
        sn = _bdot_nt(k_ref[0, pl.ds(st, MOBA_BLOCK), :], qs)
        sn = jnp.where(sel_ref[pl.ds(n, 1), :] > 0.5, sn, NEG_INF)
        return _softmax_block(sn, *carry, vt_ref[0, :, pl.ds(st, MOBA_BLOCK)])

    m, l, acc = lax.fori_loop(0, ob, past, carry)
    acc = acc / l
    row = _iota((LANES, Q_BLK), 0)
    ot = jnp.where(row < DH_A, acc[:, 0:Q_BLK], acc[:, Q_BLK:nq2])
    o_ref[0] = ot.T.astype(o_ref.dtype)


def _moba_prompt(q, kbf, vt, kmean):
    b, s, _ = q.shape
    nq = s // Q_BLK
    nblk = s // MOBA_BLOCK
    npair = D_A // LANES
    return pl.pallas_call(
        _moba_prompt_body,
        grid=(b, npair, nq),
        in_specs=[pl.BlockSpec((1, Q_BLK, LANES), lambda bi, hp, c: (bi, c, hp)),
                  pl.BlockSpec((1, s, LANES), lambda bi, hp, c: (bi, 0, hp)),
                  pl.BlockSpec((1, LANES, s), lambda bi, hp, c: (hp, 0, bi)),
                  pl.BlockSpec((1, nblk, LANES), lambda bi, hp, c: (bi, 0, hp))],
        out_specs=pl.BlockSpec((1, Q_BLK, LANES), lambda bi, hp, c: (bi, c, hp)),
        out_shape=jax.ShapeDtypeStruct((b, s, D_A), BF16),
        scratch_shapes=[pltpu.VMEM((nblk, 2 * Q_BLK), F32)],
        compiler_params=_params(("arbitrary", "arbitrary", "arbitrary")),
        name="moba_prompt",
    )(q, kbf, vt.reshape(npair, LANES, b * s), kmean)


SAMPLE_NB = 4


def _head_rows(x8):
    lane_head = _iota(x8.shape, 1) // DH_A
    return jnp.concatenate([jnp.where(lane_head == h, x8, 0.0) for h in range(H_A)], axis=0)


def _moba_sample_body(pt_ref, q_ref, kn_ref, vn_ref, *rest):
    npg = 2 * SAMPLE_NB
    kp = rest[0:npg]
    vp = rest[npg:2 * npg]
    o_ref = rest[2 * npg]
    m_scr, l_scr, ss_scr, o_scr = rest[2 * npg + 1:]
    j = pl.program_id(1)
    nsteps = pl.num_programs(1)
    nblk = m_scr.shape[0]
    lq = q_ref.shape[1]
    rows = H_A * lq
    qbd = _head_rows(q_ref[0])
    qbd_bf = qbd.astype(BF16)

    for i in range(SAMPLE_NB):
        n = j * SAMPLE_NB + i
        kblk = jnp.concatenate([kp[2 * i][0], kp[2 * i + 1][0]], axis=0)
        vblk = jnp.concatenate([vp[2 * i][0], vp[2 * i + 1][0]], axis=0)
        s = _bdot_nt(qbd_bf, kblk)
        mn = jnp.max(s, axis=1, keepdims=True)
        p = jnp.exp(s - mn)
        ln = jnp.sum(p, axis=1, keepdims=True)
        o_scr[n] = _bdot(p, vblk)
        kmean = (jnp.sum(kblk, axis=0, keepdims=True) * (1.0 / MOBA_BLOCK)).astype(BF16).astype(F32)
        sn = jnp.sum(qbd * kmean, axis=1, keepdims=True)
        m_scr[n] = jnp.broadcast_to(mn, (rows, LANES))
        l_scr[n] = jnp.broadcast_to(ln, (rows, LANES))
        ss_scr[n] = jnp.broadcast_to(sn, (rows, LANES))

    @pl.when(j == nsteps - 1)
    def _():
        ss = ss_scr[...]
        nidx = _iota(ss.shape, 0).astype(F32)
        picked = jnp.zeros(ss.shape, jnp.bool_)
        for _ in range(MOBA_TOPK):
            mx = jnp.max(ss, axis=0, keepdims=True)
            idx = jnp.min(jnp.where(ss == mx, nidx, float(nblk)), axis=0, keepdims=True)
            hit = nidx == idx
            picked = jnp.logical_or(picked, hit)
            ss = jnp.where(hit, NEG_INF, ss)
        s_own = _bdot_nt(qbd_bf, kn_ref[0])
        lrow = _iota(s_own.shape, 0) % lq
        lcol = _iota(s_own.shape, 1)
        s_own = jnp.where(lcol <= lrow, s_own, NEG_INF)
        m_own = jnp.max(s_own, axis=1, keepdims=True)
        p_own = jnp.exp(s_own - m_own)
        l_own = jnp.sum(p_own, axis=1, keepdims=True)
        o_own = _bdot(p_own, vn_ref[0])
        m_all = m_scr[...]
        m_star = jnp.maximum(jnp.max(jnp.where(picked, m_all, NEG_INF), axis=0), m_own)
        coef = jnp.where(picked, jnp.exp(m_all - m_star[None]), 0.0)
        c_own = jnp.exp(m_own - m_star)
        den = jnp.sum(coef * l_scr[...], axis=0) + c_own * l_own
        ss_scr[...] = coef
        reps = D_A // LANES

        def merge(n, acc):
            cn = ss_scr[n]
            return acc + jnp.concatenate([cn] * reps, axis=1) * o_scr[n]

        num = lax.fori_loop(0, nblk, merge, jnp.concatenate([c_own] * reps, axis=1) * o_own)
        out = num / jnp.concatenate([den] * reps, axis=1)
        lane_head = _iota((lq, D_A), 1) // DH_A
        res = jnp.zeros((lq, D_A), F32)
        for h in range(H_A):
            res = jnp.where(lane_head == h, out[h * lq:(h + 1) * lq, :], res)
        o_ref[0] = res


def _moba_sample(q, k_new, v_new, cache_k, cache_v, page_table):
    db, lq, _ = q.shape
    n_pages = page_table.shape[1]
    ppb = MOBA_BLOCK // PAGE_SIZE
    nblk = n_pages // ppb
    nsteps = nblk // SAMPLE_NB
    rows = H_A * lq

    def page_spec(pg):
        return pl.BlockSpec((1, PAGE_SIZE, D_A),
                            lambda b, j, pt: (pt[b * n_pages + j * (SAMPLE_NB * ppb) + pg], 0, 0))

    tok_spec = pl.BlockSpec((1, lq, D_A), lambda b, j, pt: (b, 0, 0))
    npg = SAMPLE_NB * ppb
    grid_spec = pltpu.PrefetchScalarGridSpec(
        num_scalar_prefetch=1,
        grid=(db, nsteps),
        in_specs=[tok_spec, tok_spec, tok_spec] + [page_spec(pg) for pg in range(npg)] * 2,
        out_specs=tok_spec,
        scratch_shapes=[pltpu.VMEM((nblk, rows, LANES), F32), pltpu.VMEM((nblk, rows, LANES), F32),
                        pltpu.VMEM((nblk, rows, LANES), F32), pltpu.VMEM((nblk, rows, D_A), F32)],
    )
    return pl.pallas_call(
        _moba_sample_body,
        grid_spec=grid_spec,
        out_shape=jax.ShapeDtypeStruct((db, lq, D_A), F32),
        compiler_params=_params(("arbitrary", "arbitrary")),
        name="moba_sample",
    )(page_table.reshape(-1), q, k_new, v_new, *([cache_k] * npg), *([cache_v] * npg))


def _shifted(x, carry8, s):
    c = x.shape[0]
    head = pltpu.roll(carry8, s, 0)
    if c > SUBLANES:
        head = jnp.concatenate([head, jnp.zeros((c - SUBLANES, x.shape[1]), x.dtype)], axis=0)
    return jnp.where(_iota(x.shape, 0) < s, head, pltpu.roll(x, s, 0))


def _cumsum_rows(g):
    c = g.shape[0]
    row = _iota(g.shape, 0)
    sh = 1
    while sh < c:
        g = g + jnp.where(row >= sh, pltpu.roll(g, sh, 0), 0.0)
        sh *= 2
    return g


def _unit_lower_inverse(mm):
    c = mm.shape[0]
    ri = _iota((c, c), 0)
    ci = _iota((c, c), 1)
    eye = (ri == ci).astype(F32)
    base = SUBLANES
    mb = jnp.where(ri // base == ci // base, mm, 0.0)
    t = eye - mb
    p = _bdot(mb, mb)
    t = t + _bdot(t, p)
    p = _bdot(p, p)
    t = t + _bdot(t, p)
    s = base
    while s < c:
        off = jnp.where(jnp.logical_and(ri // (2 * s) == ci // (2 * s), ri // s != ci // s), mm, 0.0)
        t = t - _bdot(t, _bdot(off, t))
        s *= 2
    return t


def _gdn_body(x_ref, ba_ref, z_ref, buf_ref, s0_ref, cw_ref, alog_ref, dtb_ref, gw_ref,
              o_ref, sfin_ref, carry_ref, state_ref):
    ci = pl.program_id(1)
    nchunks = pl.num_programs(1)

    @pl.when(ci == 0)
    def _():
        carry_ref[...] = buf_ref[0]
        state_ref[...] = s0_ref[0]

    x = x_ref[0]
    c = x.shape[0]
    carry8 = carry_ref[...]
    cw = cw_ref[...]
    y = (_shifted(x, carry8, 3) * cw[0:1] + _shifted(x, carry8, 2) * cw[1:2]
         + _shifted(x, carry8, 1) * cw[2:3] + x * cw[3:4])
    carry_ref[...] = x[c - SUBLANES:c, :]
    y = _silu(y)

    ba = ba_ref[0]
    beta = _sigmoid(ba)
    a = ba + dtb_ref[...]
    g = -jnp.exp(alog_ref[...]) * (jnp.maximum(a, 0.0) + jnp.log1p(jnp.exp(-jnp.abs(a))))
    gcum = _cumsum_rows(g)

    ri = _iota((c, c), 0)
    cj = _iota((c, c), 1)
    tril = ri >= cj
    strict = ri > cj
    eye = ri == cj
    gw = gw_ref[...]
    z = z_ref[0].astype(F32)
    hk = H_B * DK_B
    for h in range(H_B):
        qh = y[:, h * DK_B:(h + 1) * DK_B]
        kh = y[:, hk + h * DK_B:hk + (h + 1) * DK_B]
        vh = y[:, 2 * hk + h * DV_B:2 * hk + (h + 1) * DV_B]
        qh = qh * lax.rsqrt(jnp.sum(qh * qh, axis=-1, keepdims=True) + EPS) * (DK_B ** -0.5)
        kh = kh * lax.rsqrt(jnp.sum(kh * kh, axis=-1, keepdims=True) + EPS)
        bcol = beta[:, h:h + 1]
        gcol = gcum[:, H_B + h:H_B + h + 1]
        gcb = jnp.broadcast_to(gcol, (c, c))
        grow = jnp.sum(jnp.where(eye, gcb, 0.0), axis=0, keepdims=True)
        decay = jnp.exp(jnp.where(tril, gcb - grow, NEG_INF))
        egc = jnp.exp(gcol)
        glast = gcol[c - 1:c, :]
        kb = kh * bcol
        mm = jnp.where(strict, _bdot_nt(kb, kh) * decay, 0.0)
        rhs = jnp.concatenate([vh * bcol, kb * egc], axis=1)
        sol = _bdot(_unit_lower_inverse(mm), rhs)
        u = sol[:, 0:DV_B]
        w = sol[:, DV_B:DV_B + DK_B]
        qk = jnp.where(tril, _bdot_nt(qh, kh) * decay, 0.0)
        st = state_ref[h]
        v_new = u - _bdot(w, st)
        oh = _bdot(qh * egc, st) + _bdot(qk, v_new)
        state_ref[h] = st * jnp.exp(glast) + _bdot_tn(kh * jnp.exp(glast - gcol), v_new)
        oh = oh * lax.rsqrt(jnp.mean(oh * oh, axis=-1, keepdims=True) + EPS) * gw
        o_ref[0, :, h * DV_B:(h + 1) * DV_B] = oh * _silu(z[:, h * DV_B:(h + 1) * DV_B])

    @pl.when(ci == nchunks - 1)
    def _():
        sfin_ref[0] = state_ref[...]


def _gdn(qkvb, ba, z, buf8, s0, conv_w, alog128, dtb128, gnw, *, chunk):
    b, l, _ = qkvb.shape
    nchunks = l // chunk

    def tok_spec(n):
        return pl.BlockSpec((1, chunk, n), lambda bi, ci: (bi, ci, 0))

    return pl.pallas_call(
        _gdn_body,
        grid=(b, nchunks),
        in_specs=[tok_spec(D_CONV_B), tok_spec(LANES), tok_spec(D_B),
                  pl.BlockSpec((1, SUBLANES, D_CONV_B), lambda bi, ci: (bi, 0, 0)),
                  pl.BlockSpec((1, H_B, DK_B, DV_B), lambda bi, ci: (bi, 0, 0, 0)),
                  pl.BlockSpec((CONV_B, D_CONV_B), lambda bi, ci: (0, 0)),
                  pl.BlockSpec((1, LANES), lambda bi, ci: (0, 0)),
                  pl.BlockSpec((1, LANES), lambda bi, ci: (0, 0)),
                  pl.BlockSpec((1, DV_B), lambda bi, ci: (0, 0))],
        out_specs=[tok_spec(D_B),
                   pl.BlockSpec((1, H_B, DK_B, DV_B), lambda bi, ci: (bi, 0, 0, 0))],
        out_shape=[jax.ShapeDtypeStruct((b, l, D_B), F32),
                   jax.ShapeDtypeStruct((b, H_B, DK_B, DV_B), F32)],
        scratch_shapes=[pltpu.VMEM((SUBLANES, D_CONV_B), F32), pltpu.VMEM((H_B, DK_B, DV_B), F32)],
        compiler_params=_params(("arbitrary", "arbitrary")),
        name="gdn",
    )(qkvb, ba, z, buf8, s0, conv_w, alog128, dtb128, gnw)


def _merge_body(oa_ref, ob_ref, ga_ref, gb_ref, x_ref, mod_ref, wpa_ref, wpb_ref, wo_ref, o_ref):
    ya = _bdot(oa_ref[...], wpa_ref[...])
    yb = _bdot(ob_ref[...], wpb_ref[...])
    merged = _sigmoid(ga_ref[...].astype(F32)) * ya + _sigmoid(gb_ref[...].astype(F32)) * yb
    gt1 = mod_ref[0][:, 2 * D_MODEL:3 * D_MODEL]
    o_ref[...] = x_ref[...] + gt1 * _bdot(merged, wo_ref[...])


def _merge(oa, ob, ga, gb, x2d, mod3, wpa, wpb, wo, *, tm, rows_per_mod):
    m_rows = x2d.shape[0]
    rm = mod3.shape[1]

    def row_spec(n):
        return pl.BlockSpec((tm, n), lambda i: (i, 0))

    return pl.pallas_call(
        _merge_body,
        grid=(m_rows // tm,),
        in_specs=[row_spec(D_A), row_spec(D_B), row_spec(D_MODEL), row_spec(D_MODEL), row_spec(D_MODEL),
                  pl.BlockSpec((1, rm, 6 * D_MODEL), lambda i: (i // rows_per_mod, 0, 0)),
                  _const_spec((D_A, D_MODEL)), _const_spec((D_B, D_MODEL)), _const_spec((D_MODEL, D_MODEL))],
        out_specs=row_spec(D_MODEL),
        out_shape=jax.ShapeDtypeStruct((m_rows, D_MODEL), F32),
        compiler_params=_params(("arbitrary",)),
        name="merge",
    )(oa, ob, ga, gb, x2d, mod3, wpa, wpb, wo)


FF_CHUNK = 256


def _ffn_body(seg_len, x_ref, mod_ref, g2_ref, gf_ref, wup_ref, cw_ref, cb_ref, wdn_ref, h1_ref, h2_ref,
              o_ref, up_ref, carry_ref):
    i = pl.program_id(1)
    x = x_ref[...]
    tm = x.shape[0]
    m = mod_ref[0]
    sh2 = m[:, 3 * D_MODEL:4 * D_MODEL]
    sc2 = m[:, 4 * D_MODEL:5 * D_MODEL]
    gt2 = m[:, 5 * D_MODEL:6 * D_MODEL]
    y = x * lax.rsqrt(jnp.mean(x * x, axis=-1, keepdims=True) + EPS) * g2_ref[...]
    h = (y * (1.0 + sc2) + sh2).astype(BF16)

    if seg_len is None:
        @pl.when(i == 0)
        def _():
            carry_ref[...] = h1_ref[0]

    rowi = _iota((tm, FF_CHUNK), 0)

    def conv_act(lo):
        up = jnp.dot(h, wup_ref[:, lo:lo + FF_CHUNK], preferred_element_type=F32)
        up_ref[:, lo:lo + FF_CHUNK] = up
        if seg_len is None:
            c8 = carry_ref[:, lo:lo + FF_CHUNK]
            d1 = _shifted(up, c8, 1)
            d2 = _shifted(up, c8, 2)
            carry_ref[:, lo:lo + FF_CHUNK] = up[tm - SUBLANES:tm, :]
        else:
            lpos = rowi % seg_len
            d1 = jnp.where(lpos < 1, h1_ref[:, lo:lo + FF_CHUNK], pltpu.roll(up, 1, 0))
            d2 = jnp.where(lpos < 2, h2_ref[:, lo:lo + FF_CHUNK], pltpu.roll(up, 2, 0))
        cw = cw_ref[:, lo:lo + FF_CHUNK]
        return d2 * cw[0:1] + d1 * cw[1:2] + up * cw[2:3] + cb_ref[:, lo:lo + FF_CHUNK]

    acc = jnp.zeros((tm, D_MODEL), F32)
    for cidx in range(D_FF // FF_CHUNK):
        lo = cidx * FF_CHUNK
        act = _silu(conv_act(lo)) * conv_act(D_FF + lo)
        acc = acc + jnp.dot(act.astype(BF16), wdn_ref[lo:lo + FF_CHUNK, :], preferred_element_type=F32)
    x2 = x + gt2 * acc
    o_ref[...] = x2 * lax.rsqrt(jnp.mean(x2 * x2, axis=-1, keepdims=True) + EPS) * gf_ref[...]


def _ffn(x1, mod3, g2, gf, wup, conv_w, conv_b, wdn, h1, h2, *, nb, tm, seg_len):
    m_rows = x1.shape[0]
    nt = m_rows // (nb * tm)
    rm = mod3.shape[1]
    n_up = 2 * D_FF

    def row_spec(n):
        return pl.BlockSpec((tm, n), lambda b, i: (b * nt + i, 0))

    if seg_len is None:
        h1_spec = pl.BlockSpec((1, SUBLANES, n_up), lambda b, i: (b, 0, 0))
        h2_spec = pl.BlockSpec((1, SUBLANES, n_up), lambda b, i: (b, 0, 0))
    else:
        h1_spec = row_spec(n_up)
        h2_spec = row_spec(n_up)
    return pl.pallas_call(
        functools.partial(_ffn_body, seg_len),
        grid=(nb, nt),
        in_specs=[row_spec(D_MODEL),
                  pl.BlockSpec((1, rm, 6 * D_MODEL), lambda b, i: (b, 0, 0)),
                  _const_spec((1, D_MODEL)), _const_spec((1, D_MODEL)),
                  _const_spec((D_MODEL, n_up)), _const_spec((CONV_F, n_up)), _const_spec((1, n_up)),
                  _const_spec((D_FF, D_MODEL)), h1_spec, h2_spec],
        out_specs=[row_spec(D_MODEL), row_spec(n_up)],
        out_shape=[jax.ShapeDtypeStruct((m_rows, D_MODEL), F32),
                   jax.ShapeDtypeStruct((m_rows, n_up), F32)],
        scratch_shapes=[pltpu.VMEM((SUBLANES, n_up), F32)],
        compiler_params=_params(("arbitrary", "arbitrary")),
        name="ffn",
    )(x1, mod3, g2, gf, wup, conv_w, conv_b, wdn, h1, h2)


def _rope_tables(pos):
    half = DH_A // 2
    inv = ROPE_THETA ** (-jnp.arange(half, dtype=F32) / half)
    ang = pos.astype(F32)[:, None] * inv[None, :]
    cos, sin = jnp.cos(ang), jnp.sin(ang)
    zero = jnp.zeros_like(sin)
    reps = LANES // DH_A
    cos_t = jnp.tile(jnp.concatenate([cos, cos], axis=1), (1, reps))
    s1 = jnp.tile(jnp.concatenate([zero, sin], axis=1), (1, reps))
    s2 = jnp.tile(jnp.concatenate([-sin, zero], axis=1), (1, reps))
    return cos_t, s1, s2


def _pad_rows_front(buf, rows):
    b, r, c = buf.shape
    return jnp.concatenate([jnp.zeros((b, rows - r, c), buf.dtype), buf], axis=1)


def kernel(x_prompt, x_sample, cache_k, cache_v, state_gdn, state_gdn_conv, state_ffn_conv, page_table,
           c_prompt, c_sample, w_ada, b_ada, g_norm1, w_in, conv_b_w, a_log_b, dt_bias_b, gdn_norm_w,
           w_proj_a, w_proj_b, w_out, g_norm2, w_up, conv_f_w, conv_f_b, w_down, g_norm_f):
    bp, lp, _ = x_prompt.shape
    bs, ls, _ = x_sample.shape
    depth = w_ada.shape[0]
    assert depth == 1
    li = 0

    n_qkvz = 3 * D_A + D_CONV_B + D_B
    w = w_in[li]
    w_in_r = jnp.concatenate(
        [w[:, :n_qkvz], w[:, n_qkvz + 2 * H_B:], w[:, n_qkvz:n_qkvz + 2 * H_B],
         jnp.zeros((D_MODEL, LANES - 2 * H_B), w.dtype)], axis=1).astype(BF16)
    wpa = w_proj_a[li].astype(BF16)
    wpb = w_proj_b[li].astype(BF16)
    wo = w_out[li].astype(BF16)
    wup = w_up[li].astype(BF16)
    wdn = w_down[li].astype(BF16)
    g1 = g_norm1[li].reshape(1, D_MODEL)
    g2 = g_norm2[li].reshape(1, D_MODEL)
    gf = g_norm_f.reshape(1, D_MODEL)
    pad8 = jnp.zeros((H_B,), F32)
    alog128 = jnp.concatenate([pad8, a_log_b[li], jnp.zeros((LANES - 2 * H_B,), F32)]).reshape(1, LANES)
    dtb128 = jnp.concatenate([pad8, dt_bias_b[li], jnp.zeros((LANES - 2 * H_B,), F32)]).reshape(1, LANES)
    gnw = gdn_norm_w[li].reshape(1, DV_B)
    cfb = conv_f_b[li].reshape(1, 2 * D_FF)

    pad_rows = (-(bp + bs)) % SUBLANES
    c_all = jnp.concatenate([c_prompt, c_sample, jnp.zeros((pad_rows, D_MODEL), F32)], axis=0)
    mod = _ada(c_all, w_ada[li], b_ada[li])
    mod_p = mod[:bp].reshape(bp, 1, 6 * D_MODEL)
    mod_s = jnp.repeat(mod[bp:bp + bs], ls, axis=0).reshape(1, bs * ls, 6 * D_MODEL)

    tm_p = 512
    cos_p, s1_p, s2_p = _rope_tables(jnp.arange(lp))
    xp2 = x_prompt.reshape(bp * lp, D_MODEL)
    (q_p, k_p, kbf_p, v_p, vt_p, qkvb_p, z_p, ga_p, gb_p, ba_p, km_p) = _inproj(
        xp2, mod_p, g1, w_in_r, cos_p, s1_p, s2_p,
        tm=tm_p, n_pos_tiles=lp // tm_p, rows_per_mod=lp // tm_p, q_dtype=BF16)
    oa_p = _moba_prompt(q_p.reshape(bp, lp, D_A), kbf_p.reshape(bp, lp, D_A), vt_p,
                        km_p.reshape(bp, lp // MOBA_BLOCK, D_A))
    ob_p, sfin_p = _gdn(qkvb_p.reshape(bp, lp, D_CONV_B), ba_p.reshape(bp, lp, LANES), z_p.reshape(bp, lp, D_B),
                        jnp.zeros((bp, SUBLANES, D_CONV_B), F32), jnp.zeros((bp, H_B, DK_B, DV_B), F32),
                        conv_b_w[li], alog128, dtb128, gnw, chunk=GDN_CHUNK)
    x1_p = _merge(oa_p.reshape(bp * lp, D_A), ob_p.reshape(bp * lp, D_B), ga_p, gb_p, xp2, mod_p, wpa, wpb, wo,
                  tm=tm_p, rows_per_mod=lp // tm_p)
    tm_f = 256
    hist0 = jnp.zeros((bp, SUBLANES, 2 * D_FF), F32)
    y_p, up_p = _ffn(x1_p, mod_p, g2, gf, wup, conv_f_w[li], cfb, wdn, hist0, hist0, nb=bp, tm=tm_f, seg_len=None)

    rows_s = bs * ls
    cos_s, s1_s, s2_s = _rope_tables(PAST_LEN + jnp.arange(ls))
    xs2 = x_sample.reshape(rows_s, D_MODEL)
    (q_s, k_s, _, v_s, _, qkvb_s, z_s, ga_s, gb_s, ba_s, _) = _inproj(
        xs2, mod_s, g1, w_in_r, jnp.tile(cos_s, (bs, 1)), jnp.tile(s1_s, (bs, 1)), jnp.tile(s2_s, (bs, 1)),
        tm=rows_s, n_pos_tiles=1, rows_per_mod=1, q_dtype=F32)
    n_pool = cache_k.shape[1]
    oa_s = _moba_sample(q_s.astype(BF16).astype(F32).reshape(bs, ls, D_A), k_s.reshape(bs, ls, D_A),
                        v_s.reshape(bs, ls, D_A), cache_k[li].reshape(n_pool, PAGE_SIZE, D_A),
                        cache_v[li].reshape(n_pool, PAGE_SIZE, D_A), page_table)
    ob_s, sfin_s = _gdn(qkvb_s.reshape(bs, ls, D_CONV_B), ba_s.reshape(bs, ls, LANES), z_s.reshape(bs, ls, D_B),
                        _pad_rows_front(state_gdn_conv[li], SUBLANES), state_gdn[li],
                        conv_b_w[li], alog128, dtb128, gnw, chunk=min(GDN_CHUNK, ls))
    x1_s = _merge(oa_s.reshape(rows_s, D_A), ob_s.reshape(rows_s, D_B), ga_s, gb_s, xs2, mod_s, wpa, wpb, wo,
                  tm=rows_s, rows_per_mod=1)
    fbuf = state_ffn_conv[li]
    zrows = jnp.zeros((bs, ls - 1, 2 * D_FF), F32)
    h1_s = jnp.concatenate([fbuf[:, 1:2], zrows], axis=1).reshape(rows_s, 2 * D_FF)
    h2_s = jnp.concatenate([fbuf, zrows[:, 1:]], axis=1).reshape(rows_s, 2 * D_FF)
    y_s, up_s = _ffn(x1_s, mod_s, g2, gf, wup, conv_f_w[li], cfb, wdn, h1_s, h2_s, nb=1, tm=rows_s, seg_len=ls)

    qkvb_p3 = qkvb_p.reshape(bp, lp, D_CONV_B)
    qkvb_s3 = qkvb_s.reshape(bs, ls, D_CONV_B)
    up_p3 = up_p.reshape(bp, lp, 2 * D_FF)
    up_s3 = up_s.reshape(bs, ls, 2 * D_FF)
    return (y_p.reshape(bp, lp, D_MODEL), y_s.reshape(bs, ls, D_MODEL),
            k_p.reshape(1, bp, lp, H_A, DH_A), v_p.reshape(1, bp, lp, H_A, DH_A),
            k_s.reshape(1, bs, ls, H_A, DH_A), v_s.reshape(1, bs, ls, H_A, DH_A),
            sfin_p[None], sfin_s[None],
            qkvb_p3[None, :, lp - (CONV_B - 1):], qkvb_s3[None, :, ls - (CONV_B - 1):],
            up_p3[None, :, lp - (CONV_F - 1):], up_s3[None, :, ls - (CONV_F - 1):])
```
